```python
import jax, jax.numpy as jnp
from jax import lax
import numpy as np

D_MODEL = 4096
BATCH = 1
SEQ = 8192
DEPTH = 2

CHUNK = 64
N_MEM = 256
EPS = 1e-6

ML_HEADS = 4
ML_DV = D_MODEL // (2 * ML_HEADS)
ML_DQK = ML_DV // 2
GATE_CAP = 15.0
GLA_HEADS = 4
GLA_DV = D_MODEL // (2 * GLA_HEADS)
GLA_DK = GLA_DV // 2
GLA_RANK = 16
GLA_TAU = 16.0
ML_QK = ML_HEADS * ML_DQK
ML_V = ML_HEADS * ML_DV
GLA_QK = GLA_HEADS * GLA_DK
GLA_V = GLA_HEADS * GLA_DV
AB_IN_SIZES = (ML_QK, ML_QK, ML_V, ML_V, ML_HEADS, ML_HEADS,
               GLA_QK, GLA_QK, GLA_V, GLA_V, GLA_RANK)
AB_IN = 2 * ML_QK + 2 * ML_V + 2 * ML_HEADS + 2 * GLA_QK + 2 * GLA_V + GLA_RANK

SB_HEADS = 32
SB_DH = D_MODEL // SB_HEADS
SB_BLOCK = 128

XA_HEADS = 4
XA_DH = 256
XA_W = XA_HEADS * XA_DH

D_FF = 11008
CONV_W = 3

kernel_name = "hybrid_mlstm_gla_stickbreak_convffn"


def rmsnorm(x, g):
    xf = x.astype(jnp.float32)
    y = xf * lax.rsqrt(jnp.mean(xf * xf, axis=-1, keepdims=True) + EPS)
    return (y * g.astype(jnp.float32)).astype(x.dtype)


def softcap(x):
    return GATE_CAP * jnp.tanh(x / GATE_CAP)


def split_cols(t, sizes):
    outs, start = [], 0
    for n in sizes:
        outs.append(t[..., start:start + n])
        start += n
    return outs


def to_chunks(t, h, d):
    b, s, _ = t.shape
    return t.reshape(b, s // CHUNK, CHUNK, h, d).transpose(1, 0, 3, 2, 4)


def gates_to_chunks(t):
    b, s, h = t.shape
    return t.reshape(b, s // CHUNK, CHUNK, h).transpose(1, 0, 3, 2)


def from_chunks(t):
    nc, b, h, l, d = t.shape
    return t.transpose(1, 0, 3, 2, 4).reshape(b, nc * l, h, d)


def mlstm_chunkwise(q, k, v, i_pre, f_pre):
    f32 = jnp.float32
    q, v = q.astype(f32), v.astype(f32)
    k = k.astype(f32) * (q.shape[-1] ** -0.5)
    li = i_pre.astype(f32)
    lf = jax.nn.log_sigmoid(f_pre.astype(f32))
    _, b, h, l, dk = q.shape
    dv = v.shape[-1]
    causal = jnp.tril(jnp.ones((l, l), bool))

    def step(carry, inp):
        c, n, m = carry
        qc, kc, vc, lic, lfc = inp
        bcum = jnp.cumsum(lfc, axis=-1)
        log_d = bcum[..., :, None] - bcum[..., None, :] + lic[..., None, :]
        log_d = jnp.where(causal, log_d, -jnp.inf)
        log_inter = bcum + m[..., None]
        m_t = jnp.maximum(jnp.max(log_d, axis=-1), log_inter)
        d_mat = jnp.exp(log_d - m_t[..., None])
        inter = jnp.exp(log_inter - m_t)
        s_qk = jnp.einsum('bhtd,bhsd->bhts', qc, kc) * d_mat
        num = (jnp.einsum('bhts,bhsv->bhtv', s_qk, vc)
               + inter[..., None] * jnp.einsum('bhtd,bhdv->bhtv', qc, c))
        den = jnp.sum(s_qk, axis=-1) + inter * jnp.einsum('bhtd,bhd->bht', qc, n)
        hc = num / jnp.maximum(jnp.abs(den), jnp.exp(-m_t))[..., None]
        b_last = bcum[..., -1]
        log_w = b_last[..., None] - bcum + lic
        m_new = jnp.maximum(b_last + m, jnp.max(log_w, axis=-1))
        w = jnp.exp(log_w - m_new[..., None])
        decay = jnp.exp(b_last + m - m_new)
        c = decay[..., None, None] * c + jnp.einsum('bhs,bhsd,bhsv->bhdv', w, kc, vc)
        n = decay[..., None] * n + jnp.einsum('bhs,bhsd->bhd', w, kc)
        return (c, n, m_new), hc

    init = (jnp.zeros((b, h, dk, dv), f32), jnp.zeros((b, h, dk), f32),
            jnp.zeros((b, h), f32))
    _, hs = lax.scan(step, init, (q, k, v, li, lf))
    return hs


def gla_chunked(q, k, v, log_a):
    f32 = jnp.float32
    q = q.astype(f32) * (q.shape[-1] ** -0.5)
    k, v, log_a = k.astype(f32), v.astype(f32), log_a.astype(f32)
    _, b, h, l, dk = q.shape
    dv = v.shape[-1]
    causal = jnp.tril(jnp.ones((l, l), bool))[..., None]

    def step(s, inp):
        qc, kc, vc, lac = inp
        g = jnp.cumsum(lac, axis=-2)
        o_inter = jnp.einsum('bhtd,bhdv->bhtv', qc * jnp.exp(g), s)
        diff = jnp.where(causal, g[..., :, None, :] - g[..., None, :, :], -jnp.inf)
        a = jnp.einsum('bhtd,bhsd,bhtsd->bhts', qc, kc, jnp.exp(diff))
        o = o_inter + jnp.einsum('bhts,bhsv->bhtv', a, vc)
        g_last = g[..., -1, :]
        s = (jnp.exp(g_last)[..., None] * s
             + jnp.einsum('bhsd,bhsv->bhdv', kc * jnp.exp(g_last[..., None, :] - g), vc))
        return s, o

    _, os_ = lax.scan(step, jnp.zeros((b, h, dk, dv), f32), (q, k, v, log_a))
    return os_


def stick_breaking(q, k, v):
    b, s, h, dh = q.shape
    scale = dh ** -0.5
    outs = []
    for blk in range(s // SB_BLOCK):
        t0, t1 = blk * SB_BLOCK, (blk + 1) * SB_BLOCK
        z = jnp.einsum('bqhd,bkhd->bhqk', q[:, t0:t1], k[:, :t1]).astype(jnp.float32) * scale
        strict = jnp.arange(t1)[None, :] < jnp.arange(t0, t1)[:, None]
        log_beta = jax.nn.log_sigmoid(z)
        log_1m = jnp.where(strict, jax.nn.log_sigmoid(-z), 0.0)
        between = lax.cumsum(log_1m, axis=3, reverse=True) - log_1m
        att = jnp.exp(jnp.where(strict, log_beta + between, -jnp.inf))
        outs.append(jnp.einsum('bhqk,bkhd->bqhd', att.astype(v.dtype), v[:, :t1]))
    return jnp.concatenate(outs, axis=1)


def ab_mixer(xn, w_in, ml_i_bias, ml_f_bias, ml_head_norm, gla_w_gate, gla_gate_bias,
             gla_head_norm, w_out):
    b, s, _ = xn.shape
    mq, mk, mv, mo, mi, mf, gq, gk, gv, gg, gr = split_cols(xn @ w_in, AB_IN_SIZES)
    i_pre = softcap(mi + ml_i_bias)
    f_pre = softcap(mf + ml_f_bias)
    h_ml = mlstm_chunkwise(to_chunks(mq, ML_HEADS, ML_DQK), to_chunks(mk, ML_HEADS, ML_DQK),
                           to_chunks(mv, ML_HEADS, ML_DV), gates_to_chunks(i_pre),
                           gates_to_chunks(f_pre))
    h_ml = rmsnorm(from_chunks(h_ml).astype(xn.dtype), ml_head_norm)
    h_ml = h_ml * jax.nn.sigmoid(mo.reshape(b, s, ML_HEADS, ML_DV))
    log_a = jax.nn.log_sigmoid((gr @ gla_w_gate + gla_gate_bias).astype(jnp.float32)) / GLA_TAU
    h_gla = gla_chunked(to_chunks(gq, GLA_HEADS, GLA_DK), to_chunks(gk, GLA_HEADS, GLA_DK),
                        to_chunks(gv, GLA_HEADS, GLA_DV), to_chunks(log_a, GLA_HEADS, GLA_DK))
    h_gla = rmsnorm(from_chunks(h_gla).astype(xn.dtype), gla_head_norm)
    h_gla = h_gla * jax.nn.silu(gg.reshape(b, s, GLA_HEADS, GLA_DV))
    h = jnp.concatenate([h_ml.reshape(b, s, ML_V), h_gla.reshape(b, s, GLA_V)], axis=-1)
    return h @ w_out


def sb_mixer(xn, w_qkv, w_out):
    b, s, _ = xn.shape
    q, k, v = split_cols(xn @ w_qkv, (D_MODEL, D_MODEL, D_MODEL))
    o = stick_breaking(q.reshape(b, s, SB_HEADS, SB_DH), k.reshape(b, s, SB_HEADS, SB_DH),
                       v.reshape(b, s, SB_HEADS, SB_DH))
    return o.reshape(b, s, D_MODEL) @ w_out


def cross_attn(xn, memn, wq, wk, wv, wo):
    b, s, _ = xn.shape
    nm = memn.shape[1]
    q = (xn @ wq).reshape(b, s, XA_HEADS, XA_DH)
    k = (memn @ wk).reshape(b, nm, XA_HEADS, XA_DH)
    v = (memn @ wv).reshape(b, nm, XA_HEADS, XA_DH)
    scores = jnp.einsum('bqhd,bkhd->bhqk', q, k).astype(jnp.float32) * (XA_DH ** -0.5)
    p = jax.nn.softmax(scores, axis=-1).astype(v.dtype)
    o = jnp.einsum('bhqk,bkhd->bqhd', p, v).reshape(b, s, XA_W)
    return o @ wo


def conv_ffn(xn, w_gate, w_up, conv_w, conv_b, w_down):
    s = xn.shape[1]
    g = xn @ w_gate
    gp = jnp.pad(g, ((0, 0), (CONV_W - 1, 0), (0, 0)))
    conv = conv_b
    for j in range(CONV_W):
        conv = conv + gp[:, j:j + s] * conv_w[j]
    h = jax.nn.gelu(conv, approximate=True) * (xn @ w_up)
    return h @ w_down


def setup_inputs(seed: int = 0) -> dict:
    key = jax.random.key(seed)
    ks = iter(jax.random.split(key, 32))
    f32 = jnp.float32
    n_even = (DEPTH + 1) // 2
    n_odd = DEPTH // 2

    def nrm(shape, scale):
        return jax.random.normal(next(ks), shape, f32) * scale

    def gain(shape):
        return 1.0 + nrm(shape, 0.02)

    return {
        "x": nrm((BATCH, SEQ, D_MODEL), 1.0),
        "mem": nrm((BATCH, N_MEM, D_MODEL), 1.0),
        "mix_norm_pre": gain((DEPTH, D_MODEL)),
        "mix_norm_post": gain((DEPTH, D_MODEL)),
        "ab_w_in": nrm((n_even, D_MODEL, AB_IN), D_MODEL ** -0.5),
        "ml_i_bias": nrm((n_even, ML_HEADS), 0.1),
        "ml_f_bias": 3.0 + nrm((n_even, ML_HEADS), 0.5),
        "ml_head_norm": gain((n_even, ML_HEADS, ML_DV)),
        "gla_w_gate": nrm((n_even, GLA_RANK, GLA_QK), GLA_RANK ** -0.5),
        "gla_gate_bias": nrm((n_even, GLA_QK), 0.1),
        "gla_head_norm": gain((n_even, GLA_HEADS, GLA_DV)),
        "ab_w_out": nrm((n_even, D_MODEL, D_MODEL), D_MODEL ** -0.5),
        "sb_w_qkv": nrm((n_odd, D_MODEL, 3 * D_MODEL), D_MODEL ** -0.5),
        "sb_w_out": nrm((n_odd, D_MODEL, D_MODEL), D_MODEL ** -0.5),
        "xa_norm_pre": gain((DEPTH, D_MODEL)),
        "xa_norm_post": gain((DEPTH, D_MODEL)),
        "mem_norm": gain((DEPTH, D_MODEL)),
        "xa_wq": nrm((DEPTH, D_MODEL, XA_W), D_MODEL ** -0.5),
        "xa_wk": nrm((DEPTH, D_MODEL, XA_W), D_MODEL ** -0.5),
        "xa_wv": nrm((DEPTH, D_MODEL, XA_W), D_MODEL ** -0.5),
        "xa_wo": nrm((DEPTH, XA_W, D_MODEL), XA_W ** -0.5),
        "ffn_norm_pre": gain((DEPTH, D_MODEL)),
        "ffn_norm_post": gain((DEPTH, D_MODEL)),
        "ffn_w_gate": nrm((DEPTH, D_MODEL, D_FF), D_MODEL ** -0.5),
        "ffn_w_up": nrm((DEPTH, D_MODEL, D_FF), D_MODEL ** -0.5),
        "ffn_conv_w": nrm((DEPTH, CONV_W, D_FF), CONV_W ** -0.5),
        "ffn_conv_b": nrm((DEPTH, D_FF), 0.02),
        "ffn_w_down": nrm((DEPTH, D_FF, D_MODEL), D_FF ** -0.5),
    }


def reference(x, mem, mix_norm_pre, mix_norm_post, ab_w_in, ml_i_bias, ml_f_bias, ml_head_norm,
              gla_w_gate, gla_gate_bias, gla_head_norm, ab_w_out, sb_w_qkv, sb_w_out,
              xa_norm_pre, xa_norm_post, mem_norm, xa_wq, xa_wk, xa_wv, xa_wo,
              ffn_norm_pre, ffn_norm_post, ffn_w_gate, ffn_w_up, ffn_conv_w, ffn_conv_b,
              ffn_w_down):
    for layer in range(DEPTH):
        e = layer // 2
        h = rmsnorm(x, mix_norm_pre[layer])
        if layer % 2 == 0:
            h = ab_mixer(h, ab_w_in[e], ml_i_bias[e], ml_f_bias[e], ml_head_norm[e],
                         gla_w_gate[e], gla_gate_bias[e], gla_head_norm[e], ab_w_out[e])
        else:
            h = sb_mixer(h, sb_w_qkv[e], sb_w_out[e])
        x = x + rmsnorm(h, mix_norm_post[layer])
        h = cross_attn(rmsnorm(x, xa_norm_pre[layer]), rmsnorm(mem, mem_norm[layer]),
                       xa_wq[layer], xa_wk[layer], xa_wv[layer], xa_wo[layer])
        x = x + rmsnorm(h, xa_norm_post[layer])
        h = conv_ffn(rmsnorm(x, ffn_norm_pre[layer]), ffn_w_gate[layer], ffn_w_up[layer],
                     ffn_conv_w[layer], ffn_conv_b[layer], ffn_w_down[layer])
        x = x + rmsnorm(h, ffn_norm_post[layer])
    return x
```

```python
import functools

import jax
import jax.numpy as jnp
from jax import lax
from jax.experimental import pallas as pl
from jax.experimental.pallas import tpu as pltpu

F32 = jnp.float32
BF16 = jnp.bfloat16

EPS = 1e-6
GATE_CAP = 15.0
GLA_TAU = 16.0
CONV_W = 3

ML_HEADS = 4
GLA_HEADS = 4
GLA_RANK = 16
SB_HEADS = 32
XA_HEADS = 4

LANES = 128
SUBLANES = 8
VMEM_LIMIT = 56 * 1024 * 1024
REC_CHUNK = 128
GLA_SUB = 16
EXP_CLAMP = 80.0


def _params(*sem):
    return pltpu.CompilerParams(dimension_semantics=sem, vmem_limit_bytes=VMEM_LIMIT)


def _log_sigmoid(x):
    return jnp.minimum(x, 0.0) - jnp.log(1.0 + jnp.exp(-jnp.abs(x)))


def _sigmoid(x):
    return 1.0 / (1.0 + jnp.exp(-x))


def _softcap(x):
    return GATE_CAP * jnp.tanh(x / GATE_CAP)


def _dot(a, b):
    return jnp.dot(a, b, preferred_element_type=F32)


def _dot_nt(a, b):
    return lax.dot_general(a, b, (((1,), (1,)), ((), ())), preferred_element_type=F32)


def _dot_f32(a, b):
    return jnp.dot(a, b, preferred_element_type=F32, precision=lax.Precision.HIGHEST)


def _mm_single_kernel(a_ref, w_ref, o_ref):
    o_ref[...] = _dot(a_ref[...], w_ref[...]).astype(o_ref.dtype)


def _mm_acc_kernel(a_ref, w_ref, o_ref, acc_ref, *, nk):
    k = pl.program_id(2)

    @pl.when(k == 0)
    def _():
        acc_ref[...] = jnp.zeros_like(acc_ref)

    acc_ref[...] += _dot(a_ref[...], w_ref[...])

    @pl.when(k == nk - 1)
    def _():
        o_ref[...] = acc_ref[...].astype(o_ref.dtype)


def matmul(a, w, out_dtype, tm, tn, tk=None):
    m, kd = a.shape
    n = w.shape[1]
    tm, tn = min(tm, m), min(tn, n)
    tk = kd if tk is None else min(tk, kd)
    assert m % tm == 0 and n % tn == 0 and kd % tk == 0
    nk = kd // tk
    if nk == 1:
        return pl.pallas_call(
            _mm_single_kernel,
            grid=(m // tm, n // tn),
            in_specs=[pl.BlockSpec((tm, kd), lambda i, j: (i, 0)),
                      pl.BlockSpec((kd, tn), lambda i, j: (0, j))],
            out_specs=pl.BlockSpec((tm, tn), lambda i, j: (i, j)),
            out_shape=jax.ShapeDtypeStruct((m, n), out_dtype),
            compiler_params=_params("parallel", "arbitrary"),
            name="matmul",
        )(a, w)
    return pl.pallas_call(
        functools.partial(_mm_acc_kernel, nk=nk),
        grid=(m // tm, n // tn, nk),
        in_specs=[pl.BlockSpec((tm, tk), lambda i, j, k: (i, k)),
                  pl.BlockSpec((tk, tn), lambda i, j, k: (k, j))],
        out_specs=pl.BlockSpec((tm, tn), lambda i, j, k: (i, j)),
        out_shape=jax.ShapeDtypeStruct((m, n), out_dtype),
        scratch_shapes=[pltpu.VMEM((tm, tn), F32)],
        compiler_params=_params("parallel", "arbitrary", "arbitrary"),
        name="matmul_acc",
    )(a, w)


def _rms(x, g):
    return x * lax.rsqrt(jnp.mean(x * x, axis=-1, keepdims=True) + EPS) * g


def _norm_cast_kernel(x_ref, g_ref, o_ref):
    o_ref[...] = _rms(x_ref[...], g_ref[...]).astype(o_ref.dtype)


def norm_cast(x, g, tm=512):
    m, d = x.shape
    tm = min(tm, m)
    return pl.pallas_call(
        _norm_cast_kernel,
        grid=(m // tm,),
        in_specs=[pl.BlockSpec((tm, d), lambda i: (i, 0)),
                  pl.BlockSpec((1, d), lambda i: (0, 0))],
        out_specs=pl.BlockSpec((tm, d), lambda i: (i, 0)),
        out_shape=jax.ShapeDtypeStruct((m, d), BF16),
        compiler_params=_params("parallel"),
        name="norm_cast",
    )(x, g.reshape(1, d))


def _resid_norm_kernel(x_ref, h_ref, gpost_ref, gnext_ref, xo_ref, xn_ref):
    xnew = x_ref[...] + _rms(h_ref[...].astype(F32), gpost_ref[...])
    xo_ref[...] = xnew
    xn_ref[...] = _rms(xnew, gnext_ref[...]).astype(xn_ref.dtype)


def _resid_kernel(x_ref, h_ref, gpost_ref, xo_ref):
    xo_ref[...] = x_ref[...] + _rms(h_ref[...].astype(F32), gpost_ref[...])


def resid_norm(x, h, g_post, g_next=None, tm=256):
    m, d = x.shape
    tm = min(tm, m)
    row = pl.BlockSpec((tm, d), lambda i: (i, 0))
    vec = pl.BlockSpec((1, d), lambda i: (0, 0))
    if g_next is None:
        return pl.pallas_call(
            _resid_kernel, grid=(m // tm,),
            in_specs=[row, row, vec], out_specs=row,
            out_shape=jax.ShapeDtypeStruct((m, d), F32),
            compiler_params=_params("parallel"), name="resid",
        )(x, h, g_post.reshape(1, d))
    return pl.pallas_call(
        _resid_norm_kernel, grid=(m // tm,),
        in_specs=[row, row, vec, vec], out_specs=[row, row],
        out_shape=[jax.ShapeDtypeStruct((m, d), F32), jax.ShapeDtypeStruct((m, d), BF16)],
        compiler_params=_params("parallel"), name="resid_norm",
    )(x, h, g_post.reshape(1, d), g_next.reshape(1, d))


def _ffn_up_kernel(xn_ref, wg_ref, wu_ref, cw_ref, cb_ref, h_ref, carry_ref):
    i = pl.program_id(1)
    xn = xn_ref[...]
    g = _dot(xn, wg_ref[...])
    u = _dot(xn, wu_ref[...])
    tm = g.shape[0]

    @pl.when(i == 0)
    def _():
        carry_ref[...] = jnp.zeros_like(carry_ref)

    prev = carry_ref[...]
    p1 = prev[SUBLANES - 1:SUBLANES, :]
    p2 = prev[SUBLANES - 2:SUBLANES - 1, :]
    row = lax.broadcasted_iota(jnp.int32, g.shape, 0)
    g1 = jnp.where(row == 0, p1, pltpu.roll(g, 1, axis=0))
    g2 = jnp.where(row == 0, p2, jnp.where(row == 1, p1, pltpu.roll(g, 2, axis=0)))
    cw = cw_ref[...]
    conv = cb_ref[...] + g2 * cw[0:1, :] + g1 * cw[1:2, :] + g * cw[2:3, :]
    carry_ref[...] = g[tm - SUBLANES:tm, :]
    c = 0.7978845608028654
    gelu = 0.5 * conv * (1.0 + jnp.tanh(c * (conv + 0.044715 * (conv * conv * conv))))
    h_ref[...] = (gelu * u).astype(h_ref.dtype)


def ffn_up(xn, wg, wu, cw, cb, tm=1024, tf=512):
    m, d = xn.shape
    f = wg.shape[1]
    tm, tf = min(tm, m), min(tf, f)
    assert m % tm == 0 and f % tf == 0
    return pl.pallas_call(
        _ffn_up_kernel,
        grid=(f // tf, m // tm),
        in_specs=[pl.BlockSpec((tm, d), lambda j, i: (i, 0)),
                  pl.BlockSpec((d, tf), lambda j, i: (0, j)),
                  pl.BlockSpec((d, tf), lambda j, i: (0, j)),
                  pl.BlockSpec((CONV_W, tf), lambda j, i: (0, j)),
                  pl.BlockSpec((1, tf), lambda j, i: (0, j))],
        out_specs=pl.BlockSpec((tm, tf), lambda j, i: (i, j)),
        out_shape=jax.ShapeDtypeStruct((m, f), BF16),
        scratch_shapes=[pltpu.VMEM((SUBLANES, tf), F32)],
        compiler_params=_params("parallel", "arbitrary"),
        name="ffn_up",
    )(xn, wg, wu, cw, cb.reshape(1, f))


def _xattn_kernel(q_ref, kv_ref, o_ref, *, heads, dh):
    scale = dh ** -0.5
    for h in range(heads):
        q = q_ref[:, h * dh:(h + 1) * dh]
        k = kv_ref[:, h * dh:(h + 1) * dh]
        v = kv_ref[:, (heads + h) * dh:(heads + h + 1) * dh]
        s = _dot_nt(q, k) * scale
        s = s - jnp.max(s, axis=-1, keepdims=True)
        p = jnp.exp(s)
        p = p / jnp.sum(p, axis=-1, keepdims=True)
        o_ref[:, h * dh:(h + 1) * dh] = _dot(p.astype(BF16), v).astype(o_ref.dtype)


def xattn(q, kv, heads, tm=512):
    m, w = q.shape
    nm = kv.shape[0]
    tm = min(tm, m)
    return pl.pallas_call(
        functools.partial(_xattn_kernel, heads=heads, dh=w // heads),
        grid=(m // tm,),
        in_specs=[pl.BlockSpec((tm, w), lambda i: (i, 0)),
                  pl.BlockSpec((nm, 2 * w), lambda i: (0, 0))],
        out_specs=pl.BlockSpec((tm, w), lambda i: (i, 0)),
        out_shape=jax.ShapeDtypeStruct((m, w), BF16),
        compiler_params=_params("parallel"),
        name="xattn",
    )(q, kv)


def _sb_kernel(q_ref, k_ref, v_ref, u1_ref, o_ref, acc_ref, c_ref, *, tq, dh):
    i = pl.program_id(1)
    scale = dh ** -0.5
    q = q_ref[...]
    u1 = u1_ref[...]
    acc_ref[...] = jnp.zeros_like(acc_ref)
    c_ref[...] = jnp.zeros_like(c_ref)
    nkb = (i + 1) * (tq // LANES)
    qidx = i * tq + lax.broadcasted_iota(jnp.int32, (tq, LANES), 0)
    kloc = lax.broadcasted_iota(jnp.int32, (tq, LANES), 1)

    def body(n, carry):
        ks = pl.multiple_of((nkb - 1 - n) * LANES, LANES)
        k = k_ref[pl.ds(ks, LANES), :]
        v = v_ref[pl.ds(ks, LANES), :]
        z = _dot_nt(q, k) * scale
        strict = (kloc + ks) < qidx
        sp = jnp.log(1.0 + jnp.exp(-jnp.abs(z)))
        log_beta = jnp.minimum(z, 0.0) - sp
        l1m = jnp.where(strict, -(jnp.maximum(z, 0.0) + sp), 0.0)
        hi = l1m.astype(BF16)
        lo = (l1m - hi.astype(F32)).astype(BF16)
        bw = _dot(hi, u1) + _dot(lo, u1)
        between = bw[:, :LANES] + c_ref[...]
        att = jnp.where(strict, jnp.exp(jnp.where(strict, log_beta + between, 0.0)), 0.0)
        acc_ref[...] += _dot(att.astype(BF16), v)
        c_ref[...] += bw[:, LANES:]
        return carry

    lax.fori_loop(0, nkb, body, 0)
    o_ref[...] = acc_ref[...].astype(o_ref.dtype)


def stick_breaking(qkv, heads, tq=256):
    s = qkv.shape[0]
    dh = qkv.shape[1] // (3 * heads)
    assert dh == LANES
    tq = min(tq, s)
    j = lax.broadcasted_iota(jnp.int32, (LANES, 2 * LANES), 0)
    c = lax.broadcasted_iota(jnp.int32, (LANES, 2 * LANES), 1)
    u1 = jnp.where((c >= LANES) | (j > c), 1.0, 0.0).astype(BF16)
    return pl.pallas_call(
        functools.partial(_sb_kernel, tq=tq, dh=dh),
        grid=(heads, s // tq),
        in_specs=[pl.BlockSpec((tq, dh), lambda h, i: (i, h)),
                  pl.BlockSpec((s, dh), lambda h, i: (0, heads + h)),
                  pl.BlockSpec((s, dh), lambda h, i: (0, 2 * heads + h)),
                  pl.BlockSpec((LANES, 2 * LANES), lambda h, i: (0, 0))],
        out_specs=pl.BlockSpec((tq, dh), lambda h, i: (i, h)),
        out_shape=jax.ShapeDtypeStruct((s, heads * dh), BF16),
        scratch_shapes=[pltpu.VMEM((tq, dh), F32), pltpu.VMEM((tq, LANES), F32)],
        compiler_params=_params("parallel", "arbitrary"),
        name="stick_breaking",
    )(qkv, qkv, qkv, u1)


def _head_out(h, gain, gate):
    return h * lax.rsqrt(jnp.mean(h * h, axis=-1, keepdims=True) + EPS) * gain * gate


def _mlstm_kernel(bias_ref, q_ref, k_ref, v_ref, og_ref, gcol_ref, grow_ref, hn_ref, o_ref,
                  c_ref, n_ref, *, heads):
    h = pl.program_id(0)
    c = pl.program_id(1)

    @pl.when(c == 0)
    def _():
        c_ref[...] = jnp.zeros_like(c_ref)
        n_ref[...] = jnp.zeros_like(n_ref)

    q = q_ref[...]
    k = k_ref[...]
    v = v_ref[...]
    ln, dk = q.shape
    kscale = dk ** -0.5
    bi = bias_ref[h]
    bf = bias_ref[heads + h]
    li_col = _softcap(gcol_ref[0] + bi)
    lf_col = _log_sigmoid(_softcap(gcol_ref[1] + bf))
    li_row = _softcap(grow_ref[0] + bi)
    lf_row = _log_sigmoid(_softcap(grow_ref[1] + bf))

    row = lax.broadcasted_iota(jnp.int32, (ln, ln), 0)
    col = lax.broadcasted_iota(jnp.int32, (ln, ln), 1)
    causal = col <= row
    b_col = jnp.sum(jnp.where(causal, lf_row, 0.0), axis=1, keepdims=True)
    b_row = jnp.sum(jnp.where(row <= col, lf_col, 0.0), axis=0, keepdims=True)
    log_d = jnp.where(causal, b_col - b_row + li_row, 0.0)
    d_mat = jnp.where(causal, jnp.exp(log_d), 0.0)
    inter = jnp.exp(b_col)

    s_qk = _dot_nt(q, k) * kscale * d_mat
    n_row = n_ref[...]
    num = _dot(s_qk.astype(BF16), v) + inter * _dot(q, c_ref[...].astype(BF16))
    qn = jnp.sum(q.astype(F32) * n_row, axis=1, keepdims=True)
    den = jnp.sum(s_qk, axis=1, keepdims=True) + inter * qn
    hc = num / jnp.maximum(jnp.abs(den), 1.0)

    b_last = b_col[ln - 1:ln, :]
    w_col = jnp.exp(b_last - b_col + li_col) * kscale
    kw = k.astype(F32) * w_col
    decay = jnp.exp(b_last)
    c_ref[...] = decay * c_ref[...] + _dot(kw.T.astype(BF16), v)
    n_ref[...] = decay * n_row + jnp.sum(kw, axis=0, keepdims=True)

    gate = _sigmoid(og_ref[...].astype(F32))
    o_ref[...] = _head_out(hc, hn_ref[...], gate).astype(o_ref.dtype)


def mlstm(p, gates_col, gates_row, bias, head_norm, heads, dk, dv, offs):
    s = p.shape[0]
    ln = min(REC_CHUNK, s)
    nc = s // ln
    oq, ok, ov, oo = offs
    return pl.pallas_call(
        functools.partial(_mlstm_kernel, heads=heads),
        grid=(heads, nc),
        in_specs=[pl.BlockSpec(memory_space=pltpu.SMEM),
                  pl.BlockSpec((ln, dk), lambda h, c: (c, oq + h)),
                  pl.BlockSpec((ln, dk), lambda h, c: (c, ok + h)),
                  pl.BlockSpec((ln, dv), lambda h, c: (c, ov + h)),
                  pl.BlockSpec((ln, dv), lambda h, c: (c, oo + h)),
                  pl.BlockSpec((2, None, ln, 1), lambda h, c: (0, h, c, 0)),
                  pl.BlockSpec((2, None, None, 1, ln), lambda h, c: (0, h, c, 0, 0)),
                  pl.BlockSpec((None, 1, dv), lambda h, c: (h, 0, 0))],
        out_specs=pl.BlockSpec((ln, dv), lambda h, c: (c, h)),
        out_shape=jax.ShapeDtypeStruct((s, heads * dv), BF16),
        scratch_shapes=[pltpu.VMEM((dk, dv), F32), pltpu.VMEM((1, dk), F32)],
        compiler_params=_params("parallel", "arbitrary"),
        name="mlstm",
    )(bias, p, p, p, p, gates_col, gates_row, head_norm.reshape(heads, 1, dv))


def _gla_kernel(q_ref, k_ref, v_ref, gg_ref, gr_ref, wg_ref, gb_ref, hn_ref, o_ref, s_ref):
    c = pl.program_id(1)

    @pl.when(c == 0)
    def _():
        s_ref[...] = jnp.zeros_like(s_ref)

    ln, dk = q_ref.shape
    q = q_ref[...].astype(F32) * (dk ** -0.5)
    k = k_ref[...].astype(F32)
    v = v_ref[...]
    u = _dot_f32(gr_ref[...], wg_ref[...]) + gb_ref[...]
    log_a = _log_sigmoid(u) * (1.0 / GLA_TAU)
    row = lax.broadcasted_iota(jnp.int32, (ln, ln), 0)
    col = lax.broadcasted_iota(jnp.int32, (ln, ln), 1)
    causal = col <= row
    g = _dot_f32(jnp.where(causal, 1.0, 0.0).astype(F32), log_a)

    o = _dot((q * jnp.exp(g)).astype(BF16), s_ref[...].astype(BF16))

    blocks = []
    for r0 in range(0, ln, GLA_SUB):
        gref = g[r0 - 1:r0, :] if r0 > 0 else jnp.zeros((1, dk), F32)
        qi = q[r0:r0 + GLA_SUB, :] * jnp.exp(g[r0:r0 + GLA_SUB, :] - gref)
        ki = k * jnp.exp(jnp.minimum(gref - g, EXP_CLAMP))
        blocks.append(_dot_nt(qi.astype(BF16), ki.astype(BF16)))
    a = jnp.where(causal, jnp.concatenate(blocks, axis=0), 0.0)
    o = o + _dot(a.astype(BF16), v)

    gt = g.T
    g_last = gt[:, ln - 1:ln]
    kd = k.T * jnp.exp(g_last - gt)
    s_ref[...] = jnp.exp(g_last) * s_ref[...] + _dot(kd.astype(BF16), v)

    gg = gg_ref[...].astype(F32)
    o_ref[...] = _head_out(o, hn_ref[...], gg * _sigmoid(gg)).astype(o_ref.dtype)


def gla(p, gr, w_gate, gate_bias, head_norm, heads, dk, dv, offs):
    s = p.shape[0]
    ln = min(REC_CHUNK, s)
    nc = s // ln
    rank = gr.shape[1]
    oq, ok, ov, og = offs
    return pl.pallas_call(
        _gla_kernel,
        grid=(heads, nc),
        in_specs=[pl.BlockSpec((ln, dk), lambda h, c: (c, oq + h)),
                  pl.BlockSpec((ln, dk), lambda h, c: (c, ok + h)),
                  pl.BlockSpec((ln, dv), lambda h, c: (c, ov + h)),
                  pl.BlockSpec((ln, dv), lambda h, c: (c, og + h)),
                  pl.BlockSpec((ln, rank), lambda h, c: (c, 0)),
                  pl.BlockSpec((rank, dk), lambda h, c: (0, h)),
                  pl.BlockSpec((1, dk), lambda h, c: (0, h)),
                  pl.BlockSpec((None, 1, dv), lambda h, c: (h, 0, 0))],
        out_specs=pl.BlockSpec((ln, dv), lambda h, c: (c, h)),
        out_shape=jax.ShapeDtypeStruct((s, heads * dv), BF16),
        scratch_shapes=[pltpu.VMEM((dk, dv), F32)],
        compiler_params=_params("parallel", "arbitrary"),
        name="gla",
    )(p, p, p, p, gr, w_gate, gate_bias.reshape(1, heads * dk), head_norm.reshape(heads, 1, dv))


def ab_mixer(xn, w_in, ml_i_bias, ml_f_bias, ml_head_norm, gla_w_gate, gla_gate_bias,
             gla_head_norm, w_out):
    s, d = xn.shape
    ml_dv = d // (2 * ML_HEADS)
    ml_dk = ml_dv // 2
    gl_dv = d // (2 * GLA_HEADS)
    gl_dk = gl_dv // 2
    mqk, mv = ML_HEADS * ml_dk, ML_HEADS * ml_dv
    gqk, gv = GLA_HEADS * gl_dk, GLA_HEADS * gl_dv
    sizes = (mqk, mqk, mv, mv, ML_HEADS, ML_HEADS, gqk, gqk, gv, gv, GLA_RANK)
    starts = [0]
    for n in sizes:
        starts.append(starts[-1] + n)
    col = lambda a: w_in[:, starts[a]:starts[a + 1]]
    w_main = jnp.concatenate([col(0), col(1), col(2), col(3), col(6), col(7), col(8), col(9)],
                             axis=1).astype(BF16)
    n_small = 2 * ML_HEADS + GLA_RANK
    w_small = jnp.concatenate(
        [col(4), col(5), col(10), jnp.zeros((d, LANES - n_small), w_in.dtype)], axis=1).astype(BF16)
    p = matmul(xn, w_main, BF16, 1024, 512)
    small = matmul(xn, w_small, F32, 1024, LANES)

    ln = min(REC_CHUNK, s)
    gates = small[:, :2 * ML_HEADS].T.reshape(2, ML_HEADS, s)
    gates_col = gates.reshape(2, ML_HEADS, s, 1)
    gates_row = gates.reshape(2, ML_HEADS, s // ln, 1, ln)
    bias = jnp.concatenate([ml_i_bias, ml_f_bias]).astype(F32)
    ml_offs = (0, mqk // ml_dk, 2 * mqk // ml_dv, (2 * mqk + mv) // ml_dv)
    h_ml = mlstm(p, gates_col, gates_row, bias, ml_head_norm, ML_HEADS, ml_dk, ml_dv, ml_offs)

    g0 = 2 * mqk + 2 * mv
    gl_offs = (g0 // gl_dk, (g0 + gqk) // gl_dk, (g0 + 2 * gqk) // gl_dv,
               (g0 + 2 * gqk + gv) // gl_dv)
    gr = small[:, 2 * ML_HEADS:n_small]
    h_gla = gla(p, gr, gla_w_gate, gla_gate_bias, gla_head_norm, GLA_HEADS, gl_dk, gl_dv, gl_offs)
    h = jnp.concatenate([h_ml, h_gla], axis=1)
    return matmul(h, w_out.astype(BF16), F32, 1024, 512)


def sb_mixer(xn, w_qkv, w_out):
    qkv = matmul(xn, w_qkv.astype(BF16), BF16, 1024, 512)
    o = stick_breaking(qkv, SB_HEADS)
    return matmul(o, w_out.astype(BF16), F32, 1024, 512)


def cross_attn(xn, memn, wq, wk, wv, wo):
    q = matmul(xn, wq.astype(BF16), BF16, 1024, 512)
    kv = matmul(memn, jnp.concatenate([wk, wv], axis=1).astype(BF16), BF16, 256, 512)
    o = xattn(q, kv, XA_HEADS)
    return matmul(o, wo.astype(BF16), F32, 1024, 512)


def conv_ffn(xn, w_gate, w_up, conv_w, conv_b, w_down, tf=512):
    f = w_gate.shape[1]
    pad = (-f) % tf
    wg = jnp.pad(w_gate, ((0, 0), (0, pad))).astype(BF16)
    wu = jnp.pad(w_up, ((0, 0), (0, pad))).astype(BF16)
    cw = jnp.pad(conv_w, ((0, 0), (0, pad)))
    cb = jnp.pad(conv_b, ((0, pad),))
    wd = jnp.pad(w_down, ((0, pad), (0, 0))).astype(BF16)
    h = ffn_up(xn, wg, wu, cw, cb, tf=tf)
    return matmul(h, wd, F32, 1024, 2048, 1024)


def kernel(x, mem, mix_norm_pre, mix_norm_post, ab_w_in, ml_i_bias, ml_f_bias, ml_head_norm,
           gla_w_gate, gla_gate_bias, gla_head_norm, ab_w_out, sb_w_qkv, sb_w_out,
           xa_norm_pre, xa_norm_post, mem_norm, xa_wq, xa_wk, xa_wv, xa_wo,
           ffn_norm_pre, ffn_norm_post, ffn_w_gate, ffn_w_up, ffn_conv_w, ffn_conv_b,
           ffn_w_down):
    b, s, d = x.shape
    depth = mix_norm_pre.shape[0]
    outs = []
    for bi in range(b):
        xb = x[bi]
        memb = mem[bi]
        xn = norm_cast(xb, mix_norm_pre[0])
        for layer in range(depth):
            e = layer // 2
            if layer % 2 == 0:
                h = ab_mixer(xn, ab_w_in[e], ml_i_bias[e], ml_f_bias[e], ml_head_norm[e],
                             gla_w_gate[e], gla_gate_bias[e], gla_head_norm[e], ab_w_out[e])
            else:
                h = sb_mixer(xn, sb_w_qkv[e], sb_w_out[e])
            xb, xn = resid_norm(xb, h, mix_norm_post[layer], xa_norm_pre[layer])
            memn = norm_cast(memb, mem_norm[layer])
            h = cross_attn(xn, memn, xa_wq[layer], xa_wk[layer], xa_wv[layer], xa_wo[layer])
            xb, xn = resid_norm(xb, h, xa_norm_post[layer], ffn_norm_pre[layer])
            h = conv_ffn(xn, ffn_w_gate[layer], ffn_w_up[layer], ffn_conv_w[layer],
                         ffn_conv_b[layer], ffn_w_down[layer])
            if layer + 1 < depth:
                xb, xn = resid_norm(xb, h, ffn_norm_post[layer], mix_norm_pre[layer + 1])
            else:
                xb = resid_norm(xb, h, ffn_norm_post[layer])
        outs.append(xb)
    return jnp.stack(outs, axis=0)
```

```python
import functools

import jax
import jax.numpy as jnp
from jax import lax
from jax.experimental import pallas as pl
from jax.experimental.pallas import tpu as pltpu

F32 = jnp.float32
BF16 = jnp.bfloat16

EPS = 1e-6
GATE_CAP = 15.0
GLA_TAU = 16.0
CONV_W = 3

ML_HEADS = 4
GLA_HEADS = 4
GLA_RANK = 16
SB_HEADS = 32
XA_HEADS = 4

LANES = 128
SUBLANES = 8
VMEM_LIMIT = 56 * 1024 * 1024
REC_CHUNK = 128
GLA_SUB = 16
EXP_CLAMP = 80.0


def _params(*sem):
    return pltpu.CompilerParams(dimension_semantics=sem, vmem_limit_bytes=VMEM_LIMIT)


def _log_sigmoid(x):
    return jnp.minimum(x, 0.0) - jnp.log(1.0 + jnp.exp(-jnp.abs(x)))


def _sigmoid(x):
    return 1.0 / (1.0 + jnp.exp(-x))


def _softcap(x):
    return GATE_CAP * jnp.tanh(x / GATE_CAP)


def _dot(a, b):
    return jnp.dot(a, b, preferred_element_type=F32)


def _dot_nt(a, b):
    return lax.dot_general(a, b, (((1,), (1,)), ((), ())), preferred_element_type=F32)


def _dot_f32(a, b):
    return jnp.dot(a, b, preferred_element_type=F32, precision=lax.Precision.HIGHEST)


def _mm_single_kernel(a_ref, w_ref, o_ref):
    o_ref[...] = _dot(a_ref[...], w_ref[...].astype(BF16)).astype(o_ref.dtype)


def _mm_scaled_kernel(a_ref, w_ref, s_ref, o_ref):
    o_ref[...] = _dot(a_ref[...], (w_ref[...] * s_ref[...]).astype(BF16)).astype(o_ref.dtype)


def _mm_acc_kernel(a_ref, w_ref, o_ref, acc_ref, *, nk):
    k = pl.program_id(2)

    @pl.when(k == 0)
    def _():
        acc_ref[...] = jnp.zeros_like(acc_ref)

    acc_ref[...] += _dot(a_ref[...], w_ref[...].astype(BF16))

    @pl.when(k == nk - 1)
    def _():
        o_ref[...] = acc_ref[...].astype(o_ref.dtype)


def matmul(a, w, out_dtype, tm, tn, tk=None, col_scale=None):
    m, kd = a.shape
    n = w.shape[1]
    tm, tn = min(tm, m), min(tn, n)
    tk = kd if tk is None else min(tk, kd)
    assert m % tm == 0 and n % tn == 0 and kd % tk == 0
    nk = kd // tk
    if nk == 1:
        specs = [pl.BlockSpec((tm, kd), lambda i, j: (i, 0)),
                 pl.BlockSpec((kd, tn), lambda i, j: (0, j))]
        args = (a, w)
        body = _mm_single_kernel
        if col_scale is not None:
            specs.append(pl.BlockSpec((1, tn), lambda i, j: (0, j)))
            args = (a, w, col_scale.reshape(1, n).astype(w.dtype))
            body = _mm_scaled_kernel
        return pl.pallas_call(
            body,
            grid=(m // tm, n // tn),
            in_specs=specs,
            out_specs=pl.BlockSpec((tm, tn), lambda i, j: (i, j)),
            out_shape=jax.ShapeDtypeStruct((m, n), out_dtype),
            compiler_params=_params("parallel", "arbitrary"),
            name="matmul",
        )(*args)
    assert col_scale is None
    return pl.pallas_call(
        functools.partial(_mm_acc_kernel, nk=nk),
        grid=(m // tm, n // tn, nk),
        in_specs=[pl.BlockSpec((tm, tk), lambda i, j, k: (i, k)),
                  pl.BlockSpec((tk, tn), lambda i, j, k: (k, j))],
        out_specs=pl.BlockSpec((tm, tn), lambda i, j, k: (i, j)),
        out_shape=jax.ShapeDtypeStruct((m, n), out_dtype),
        scratch_shapes=[pltpu.VMEM((tm, tn), F32)],
        compiler_params=_params("parallel", "arbitrary", "arbitrary"),
        name="matmul_acc",
    )(a, w)


def _rms(x, g):
    return x * lax.rsqrt(jnp.mean(x * x, axis=-1, keepdims=True) + EPS) * g


def _norm_cast_kernel(x_ref, g_ref, o_ref):
    o_ref[...] = _rms(x_ref[...], g_ref[...]).astype(o_ref.dtype)


def norm_cast(x, g, tm=512):
    m, d = x.shape
    tm = min(tm, m)
    return pl.pallas_call(
        _norm_cast_kernel,
        grid=(m // tm,),
        in_specs=[pl.BlockSpec((tm, d), lambda i: (i, 0)),
                  pl.BlockSpec((1, d), lambda i: (0, 0))],
        out_specs=pl.BlockSpec((tm, d), lambda i: (i, 0)),
        out_shape=jax.ShapeDtypeStruct((m, d), BF16),
        compiler_params=_params("parallel"),
        name="norm_cast",
    )(x, g.reshape(1, d))


def _resid_norm_kernel(x_ref, h_ref, gpost_ref, gnext_ref, xo_ref, xn_ref):
    xnew = x_ref[...] + _rms(h_ref[...].astype(F32), gpost_ref[...])
    xo_ref[...] = xnew
    xn_ref[...] = _rms(xnew, gnext_ref[...]).astype(xn_ref.dtype)


def _resid_kernel(x_ref, h_ref, gpost_ref, xo_ref):
    xo_ref[...] = x_ref[...] + _rms(h_ref[...].astype(F32), gpost_ref[...])


def resid_norm(x, h, g_post, g_next=None, tm=256):
    m, d = x.shape
    tm = min(tm, m)
    row = pl.BlockSpec((tm, d), lambda i: (i, 0))
    vec = pl.BlockSpec((1, d), lambda i: (0, 0))
    if g_next is None:
        return pl.pallas_call(
            _resid_kernel, grid=(m // tm,),
            in_specs=[row, row, vec], out_specs=row,
            out_shape=jax.ShapeDtypeStruct((m, d), F32),
            compiler_params=_params("parallel"), name="resid",
        )(x, h, g_post.reshape(1, d))
    return pl.pallas_call(
        _resid_norm_kernel, grid=(m // tm,),
        in_specs=[row, row, vec, vec], out_specs=[row, row],
        out_shape=[jax.ShapeDtypeStruct((m, d), F32), jax.ShapeDtypeStruct((m, d), BF16)],
        compiler_params=_params("parallel"), name="resid_norm",
    )(x, h, g_post.reshape(1, d), g_next.reshape(1, d))


FFN_SUBTILES = 4


def _ffn_up_kernel(xn_ref, wg_ref, wu_ref, cw_ref, cb_ref, h_ref, carry_ref, wgb_ref, wub_ref):
    i = pl.program_id(1)

    @pl.when(i == 0)
    def _():
        carry_ref[...] = jnp.zeros_like(carry_ref)
        wgb_ref[...] = wg_ref[...].astype(BF16)
        wub_ref[...] = wu_ref[...].astype(BF16)

    tm = xn_ref.shape[0]
    ts = tm // FFN_SUBTILES
    cw = cw_ref[...]
    cb = cb_ref[...]
    prev = carry_ref[...]
    row = lax.broadcasted_iota(jnp.int32, (ts, cw.shape[1]), 0)
    for r in range(FFN_SUBTILES):
        xn = xn_ref[r * ts:(r + 1) * ts, :]
        g = _dot(xn, wgb_ref[...])
        p1 = prev[SUBLANES - 1:SUBLANES, :]
        p2 = prev[SUBLANES - 2:SUBLANES - 1, :]
        g1 = jnp.where(row == 0, p1, pltpu.roll(g, 1, axis=0))
        g2 = jnp.where(row == 0, p2, jnp.where(row == 1, p1, pltpu.roll(g, 2, axis=0)))
        conv = cb + g2 * cw[0:1, :] + g1 * cw[1:2, :] + g * cw[2:3, :]
        prev = g[ts - SUBLANES:ts, :]
        c = 0.7978845608028654
        gelu = 0.5 * conv * (1.0 + jnp.tanh(c * (conv + 0.044715 * (conv * conv * conv))))
        u = _dot(xn, wub_ref[...])
        h_ref[r * ts:(r + 1) * ts, :] = (gelu * u).astype(h_ref.dtype)
    carry_ref[...] = prev


def ffn_up(xn, wg, wu, cw, cb, tm=1024, tf=256):
    m, d = xn.shape
    f = wg.shape[1]
    tm, tf = min(tm, m), min(tf, f)
    assert m % tm == 0 and f % tf == 0
    return pl.pallas_call(
        _ffn_up_kernel,
        grid=(f // tf, m // tm),
        in_specs=[pl.BlockSpec((tm, d), lambda j, i: (i, 0)),
                  pl.BlockSpec((d, tf), lambda j, i: (0, j)),
                  pl.BlockSpec((d, tf), lambda j, i: (0, j)),
                  pl.BlockSpec((CONV_W, tf), lambda j, i: (0, j)),
                  pl.BlockSpec((1, tf), lambda j, i: (0, j))],
        out_specs=pl.BlockSpec((tm, tf), lambda j, i: (i, j)),
        out_shape=jax.ShapeDtypeStruct((m, f), BF16),
        scratch_shapes=[pltpu.VMEM((SUBLANES, tf), F32), pltpu.VMEM((d, tf), BF16),
                        pltpu.VMEM((d, tf), BF16)],
        compiler_params=_params("parallel", "arbitrary"),
        name="ffn_up",
    )(xn, wg, wu, cw, cb.reshape(1, f))


def _xattn_kernel(q_ref, kv_ref, o_ref, *, heads, dh):
    scale = dh ** -0.5
    for h in range(heads):
        q = q_ref[:, h * dh:(h + 1) * dh]
        k = kv_ref[:, h * dh:(h + 1) * dh]
        v = kv_ref[:, (heads + h) * dh:(heads + h + 1) * dh]
        s = _dot_nt(q, k) * scale
        s = s - jnp.max(s, axis=-1, keepdims=True)
        p = jnp.exp(s)
        p = p / jnp.sum(p, axis=-1, keepdims=True)
        o_ref[:, h * dh:(h + 1) * dh] = _dot(p.astype(BF16), v).astype(o_ref.dtype)


def xattn(q, kv, heads, tm=512):
    m, w = q.shape
    nm = kv.shape[0]
    tm = min(tm, m)
    return pl.pallas_call(
        functools.partial(_xattn_kernel, heads=heads, dh=w // heads),
        grid=(m // tm,),
        in_specs=[pl.BlockSpec((tm, w), lambda i: (i, 0)),
                  pl.BlockSpec((nm, 2 * w), lambda i: (0, 0))],
        out_specs=pl.BlockSpec((tm, w), lambda i: (i, 0)),
        out_shape=jax.ShapeDtypeStruct((m, w), BF16),
        compiler_params=_params("parallel"),
        name="xattn",
    )(q, kv)


SB_TQ = 512
SB_TK = 256
LOG2E = 1.4426950408889634


def _sb_group(z, v, uu, c, mask):
    neg_abs = lax.bitcast_convert_type(
        lax.bitcast_convert_type(z, jnp.uint32) | jnp.uint32(0x80000000), F32)
    sp = jnp.log(1.0 + jnp.exp2(neg_abs)) * LOG2E
    nl = jnp.maximum(z, 0.0) + sp
    log_beta = z - nl
    if mask is not None:
        nl = jnp.where(mask, nl, 0.0)
    between = _dot(nl.astype(BF16), uu)
    att = jnp.exp2(log_beta + between + c)
    if mask is not None:
        att = jnp.where(mask, att, 0.0)
    c_new = c + between[:, 0:1] - nl[:, 0:1]
    return _dot(att.astype(BF16), v), c_new


def _sb_kernel(q_ref, k_ref, v_ref, uu_ref, o_ref, acc_ref, c_ref, z_ref, *, tq, tk):
    i = pl.program_id(1)
    q = q_ref[...]
    uu = uu_ref[...]
    ngd = tq // tk
    row = lax.broadcasted_iota(jnp.int32, (tq, tk), 0)
    col = lax.broadcasted_iota(jnp.int32, (tq, tk), 1)

    def scores(ks):
        return _dot_nt(q, k_ref[pl.ds(ks, tk), :])

    acc = jnp.zeros(acc_ref.shape, F32)
    c = jnp.zeros(c_ref.shape, F32)
    for gd in reversed(range(ngd)):
        ks = pl.multiple_of(i * tq + gd * tk, tk)
        pv, c = _sb_group(scores(ks), v_ref[pl.ds(ks, tk), :], uu, c, (col + gd * tk) < row)
        acc = acc + pv
    acc_ref[...] = acc
    c_ref[...] = c

    def start(g):
        return pl.multiple_of(jnp.maximum(i * ngd - 1 - g, 0) * tk, tk)

    for u in range(ngd):
        z_ref[u] = scores(start(u))

    def body(n, carry):
        c = c_ref[...]
        pv = None
        for u in range(ngd):
            z = z_ref[u]
            z_ref[u] = scores(start((n + 1) * ngd + u))
            pvu, c = _sb_group(z, v_ref[pl.ds(start(n * ngd + u), tk), :], uu, c, None)
            pv = pvu if pv is None else pv + pvu
        acc_ref[...] += pv
        c_ref[...] = c
        return carry

    lax.fori_loop(0, i, body, 0)
    o_ref[...] = acc_ref[...].astype(o_ref.dtype)


def stick_breaking(qkv, heads):
    s = qkv.shape[0]
    dh = qkv.shape[1] // (3 * heads)
    assert dh == LANES
    tq = min(SB_TQ, s)
    tk = min(SB_TK, tq)
    j = lax.broadcasted_iota(jnp.int32, (tk, tk), 0)
    c = lax.broadcasted_iota(jnp.int32, (tk, tk), 1)
    uu = jnp.where(j > c, -1.0, 0.0).astype(BF16)
    return pl.pallas_call(
        functools.partial(_sb_kernel, tq=tq, tk=tk),
        grid=(heads, s // tq),
        in_specs=[pl.BlockSpec((tq, dh), lambda h, i: (i, h)),
                  pl.BlockSpec((s, dh), lambda h, i: (0, heads + h)),
                  pl.BlockSpec((s, dh), lambda h, i: (0, 2 * heads + h)),
                  pl.BlockSpec((tk, tk), lambda h, i: (0, 0))],
        out_specs=pl.BlockSpec((tq, dh), lambda h, i: (i, h)),
        out_shape=jax.ShapeDtypeStruct((s, heads * dh), BF16),
        scratch_shapes=[pltpu.VMEM((tq, dh), F32), pltpu.VMEM((tq, 1), F32),
                        pltpu.VMEM((tq // tk, tq, tk), F32)],
        compiler_params=_params("parallel", "arbitrary"),
        name="stick_breaking",
    )(qkv, qkv, qkv, uu)


def _head_out(h, gain, gate):
    return h * lax.rsqrt(jnp.mean(h * h, axis=-1, keepdims=True) + EPS) * gain * gate


def _mlstm_kernel(bias_ref, q_ref, k_ref, v_ref, og_ref, gcol_ref, grow_ref, hn_ref, o_ref,
                  c_ref, n_ref, *, heads):
    h = pl.program_id(0)
    c = pl.program_id(1)

    @pl.when(c == 0)
    def _():
        c_ref[...] = jnp.zeros_like(c_ref)
        n_ref[...] = jnp.zeros_like(n_ref)

    q = q_ref[...]
    k = k_ref[...]
    v = v_ref[...]
    ln, dk = q.shape
    kscale = dk ** -0.5
    bi = bias_ref[h]
    bf = bias_ref[heads + h]
    li_col = _softcap(gcol_ref[0] + bi)
    lf_col = _log_sigmoid(_softcap(gcol_ref[1] + bf))
    li_row = _softcap(grow_ref[0] + bi)
    lf_row = _log_sigmoid(_softcap(grow_ref[1] + bf))

    row = lax.broadcasted_iota(jnp.int32, (ln, ln), 0)
    col = lax.broadcasted_iota(jnp.int32, (ln, ln), 1)
    causal = col <= row
    b_col = jnp.sum(jnp.where(causal, lf_row, 0.0), axis=1, keepdims=True)
    b_row = jnp.sum(jnp.where(row <= col, lf_col, 0.0), axis=0, keepdims=True)
    log_d = jnp.where(causal, b_col - b_row + li_row, 0.0)
    d_mat = jnp.where(causal, jnp.exp(log_d), 0.0)
    inter = jnp.exp(b_col)

    s_qk = _dot_nt(q, k) * kscale * d_mat
    n_row = n_ref[...]
    num = _dot(s_qk.astype(BF16), v) + inter * _dot(q, c_ref[...].astype(BF16))
    qn = jnp.sum(q.astype(F32) * n_row, axis=1, keepdims=True)
    den = jnp.sum(s_qk, axis=1, keepdims=True) + inter * qn
    hc = num / jnp.maximum(jnp.abs(den), 1.0)

    b_last = b_col[ln - 1:ln, :]
    w_col = jnp.exp(b_last - b_col + li_col) * kscale
    kw = k.astype(F32) * w_col
    decay = jnp.exp(b_last)
    c_ref[...] = decay * c_ref[...] + _dot(kw.T.astype(BF16), v)
    n_ref[...] = decay * n_row + jnp.sum(kw, axis=0, keepdims=True)

    gate = _sigmoid(og_ref[...].astype(F32))
    o_ref[...] = _head_out(hc, hn_ref[...], gate).astype(o_ref.dtype)


def mlstm(p, gates_col, gates_row, bias, head_norm, heads, dk, dv, offs):
    s = p.shape[0]
    ln = min(REC_CHUNK, s)
    nc = s // ln
    oq, ok, ov, oo = offs
    return pl.pallas_call(
        functools.partial(_mlstm_kernel, heads=heads),
        grid=(heads, nc),
        in_specs=[pl.BlockSpec(memory_space=pltpu.SMEM),
                  pl.BlockSpec((ln, dk), lambda h, c: (c, oq + h)),
                  pl.BlockSpec((ln, dk), lambda h, c: (c, ok + h)),
                  pl.BlockSpec((ln, dv), lambda h, c: (c, ov + h)),
                  pl.BlockSpec((ln, dv), lambda h, c: (c, oo + h)),
                  pl.BlockSpec((2, None, ln, 1), lambda h, c: (0, h, c, 0)),
                  pl.BlockSpec((2, None, None, 1, ln), lambda h, c: (0, h, c, 0, 0)),
                  pl.BlockSpec((None, 1, dv), lambda h, c: (h, 0, 0))],
        out_specs=pl.BlockSpec((ln, dv), lambda h, c: (c, h)),
        out_shape=jax.ShapeDtypeStruct((s, heads * dv), BF16),
        scratch_shapes=[pltpu.VMEM((dk, dv), F32), pltpu.VMEM((1, dk), F32)],
        compiler_params=_params("parallel", "arbitrary"),
        name="mlstm",
    )(bias, p, p, p, p, gates_col, gates_row, head_norm.reshape(heads, 1, dv))


def _gla_kernel(q_ref, k_ref, v_ref, gg_ref, gr_ref, wg_ref, gb_ref, hn_ref, o_ref, s_ref):
    c = pl.program_id(1)

    @pl.when(c == 0)
    def _():
        s_ref[...] = jnp.zeros_like(s_ref)

    ln, dk = q_ref.shape
    q = q_ref[...].astype(F32) * (dk ** -0.5)
    k = k_ref[...].astype(F32)
    v = v_ref[...]
    u = _dot_f32(gr_ref[...], wg_ref[...]) + gb_ref[...]
    log_a = _log_sigmoid(u) * (1.0 / GLA_TAU)
    row = lax.broadcasted_iota(jnp.int32, (ln, ln), 0)
    col = lax.broadcasted_iota(jnp.int32, (ln, ln), 1)
    causal = col <= row
    g = _dot_f32(jnp.where(causal, 1.0, 0.0).astype(F32), log_a)

    o = _dot((q * jnp.exp(g)).astype(BF16), s_ref[...].astype(BF16))

    blocks = []
    for r0 in range(0, ln, GLA_SUB):
        gref = g[r0 - 1:r0, :] if r0 > 0 else jnp.zeros((1, dk), F32)
        qi = q[r0:r0 + GLA_SUB, :] * jnp.exp(g[r0:r0 + GLA_SUB, :] - gref)
        ki = k * jnp.exp(jnp.minimum(gref - g, EXP_CLAMP))
        blocks.append(_dot_nt(qi.astype(BF16), ki.astype(BF16)))
    a = jnp.where(causal, jnp.concatenate(blocks, axis=0), 0.0)
    o = o + _dot(a.astype(BF16), v)

    gt = g.T
    g_last = gt[:, ln - 1:ln]
    kd = k.T * jnp.exp(g_last - gt)
    s_ref[...] = jnp.exp(g_last) * s_ref[...] + _dot(kd.astype(BF16), v)

    gg = gg_ref[...].astype(F32)
    o_ref[...] = _head_out(o, hn_ref[...], gg * _sigmoid(gg)).astype(o_ref.dtype)


def gla(p, gr, w_gate, gate_bias, head_norm, heads, dk, dv, offs):
    s = p.shape[0]
    ln = min(REC_CHUNK, s)
    nc = s // ln
    rank = gr.shape[1]
    oq, ok, ov, og = offs
    return pl.pallas_call(
        _gla_kernel,
        grid=(heads, nc),
        in_specs=[pl.BlockSpec((ln, dk), lambda h, c: (c, oq + h)),
                  pl.BlockSpec((ln, dk), lambda h, c: (c, ok + h)),
                  pl.BlockSpec((ln, dv), lambda h, c: (c, ov + h)),
                  pl.BlockSpec((ln, dv), lambda h, c: (c, og + h)),
                  pl.BlockSpec((ln, rank), lambda h, c: (c, 0)),
                  pl.BlockSpec((rank, dk), lambda h, c: (0, h)),
                  pl.BlockSpec((1, dk), lambda h, c: (0, h)),
                  pl.BlockSpec((None, 1, dv), lambda h, c: (h, 0, 0))],
        out_specs=pl.BlockSpec((ln, dv), lambda h, c: (c, h)),
        out_shape=jax.ShapeDtypeStruct((s, heads * dv), BF16),
        scratch_shapes=[pltpu.VMEM((dk, dv), F32)],
        compiler_params=_params("parallel", "arbitrary"),
        name="gla",
    )(p, p, p, p, gr, w_gate, gate_bias.reshape(1, heads * dk), head_norm.reshape(heads, 1, dv))


def ab_mixer(xn, w_in, ml_i_bias, ml_f_bias, ml_head_norm, gla_w_gate, gla_gate_bias,
             gla_head_norm, w_out):
    s, d = xn.shape
    ml_dv = d // (2 * ML_HEADS)
    ml_dk = ml_dv // 2
    gl_dv = d // (2 * GLA_HEADS)
    gl_dk = gl_dv // 2
    mqk, mv = ML_HEADS * ml_dk, ML_HEADS * ml_dv
    gqk, gv = GLA_HEADS * gl_dk, GLA_HEADS * gl_dv
    sizes = (mqk, mqk, mv, mv, ML_HEADS, ML_HEADS, gqk, gqk, gv, gv, GLA_RANK)
    starts = [0]
    for n in sizes:
        starts.append(starts[-1] + n)
    col = lambda a: w_in[:, starts[a]:starts[a + 1]]
    w_main = jnp.concatenate([col(0), col(1), col(2), col(3), col(6), col(7), col(8), col(9)],
                             axis=1).astype(BF16)
    n_small = 2 * ML_HEADS + GLA_RANK
    w_small = jnp.concatenate(
        [col(4), col(5), col(10), jnp.zeros((d, LANES - n_small), w_in.dtype)], axis=1).astype(BF16)
    p = matmul(xn, w_main, BF16, 1024, 512)
    small = matmul(xn, w_small, F32, 1024, LANES)

    ln = min(REC_CHUNK, s)
    gates = small[:, :2 * ML_HEADS].T.reshape(2, ML_HEADS, s)
    gates_col = gates.reshape(2, ML_HEADS, s, 1)
    gates_row = gates.reshape(2, ML_HEADS, s // ln, 1, ln)
    bias = jnp.concatenate([ml_i_bias, ml_f_bias]).astype(F32)
    ml_offs = (0, mqk // ml_dk, 2 * mqk // ml_dv, (2 * mqk + mv) // ml_dv)
    h_ml = mlstm(p, gates_col, gates_row, bias, ml_head_norm, ML_HEADS, ml_dk, ml_dv, ml_offs)

    g0 = 2 * mqk + 2 * mv
    gl_offs = (g0 // gl_dk, (g0 + gqk) // gl_dk, (g0 + 2 * gqk) // gl_dv,
               (g0 + 2 * gqk + gv) // gl_dv)
    gr = small[:, 2 * ML_HEADS:n_small]
    h_gla = gla(p, gr, gla_w_gate, gla_gate_bias, gla_head_norm, GLA_HEADS, gl_dk, gl_dv, gl_offs)
    h = jnp.concatenate([h_ml, h_gla], axis=1)
    return matmul(h, w_out, F32, 1024, 512)


def sb_mixer(xn, w_qkv, w_out):
    d = xn.shape[1]
    qscale = (d // SB_HEADS) ** -0.5 * LOG2E
    colscale = jnp.where(jnp.arange(3 * d) < d, qscale, 1.0)
    qkv = matmul(xn, w_qkv, BF16, 1024, 512, col_scale=colscale)
    o = stick_breaking(qkv, SB_HEADS)
    return matmul(o, w_out, F32, 1024, 512)


def cross_attn(xn, memn, wq, wk, wv, wo):
    q = matmul(xn, wq, BF16, 1024, 512)
    kv = matmul(memn, jnp.concatenate([wk, wv], axis=1), BF16, 256, 512)
    o = xattn(q, kv, XA_HEADS)
    return matmul(o, wo, F32, 1024, 512)


def _down_tk(f):
    best = LANES
    for t in range(LANES, f // 2 + 1, LANES):
        if f % t == 0:
            best = t
    return best if f > 2048 else f


def conv_ffn(xn, w_gate, w_up, conv_w, conv_b, w_down):
    f = w_gate.shape[1]
    h = ffn_up(xn, w_gate, w_up, conv_w, conv_b)
    return matmul(h, w_down.astype(BF16), F32, 1024, 512, _down_tk(f))


def kernel(x, mem, mix_norm_pre, mix_norm_post, ab_w_in, ml_i_bias, ml_f_bias, ml_head_norm,
           gla_w_gate, gla_gate_bias, gla_head_norm, ab_w_out, sb_w_qkv, sb_w_out,
           xa_norm_pre, xa_norm_post, mem_norm, xa_wq, xa_wk, xa_wv, xa_wo,
           ffn_norm_pre, ffn_norm_post, ffn_w_gate, ffn_w_up, ffn_conv_w, ffn_conv_b,
           ffn_w_down):
    b, s, d = x.shape
    depth = mix_norm_pre.shape[0]
    outs = []
    for bi in range(b):
        xb = x[bi]
        memb = mem[bi]
        xn = norm_cast(xb, mix_norm_pre[0])
        for layer in range(depth):
            e = layer // 2
            if layer % 2 == 0:
                h = ab_mixer(xn, ab_w_in[e], ml_i_bias[e], ml_f_bias[e], ml_head_norm[e],
                             gla_w_gate[e], gla_gate_bias[e], gla_head_norm[e], ab_w_out[e])
            else:
                h = sb_mixer(xn, sb_w_qkv[e], sb_w_out[e])
            xb, xn = resid_norm(xb, h, mix_norm_post[layer], xa_norm_pre[layer])
            memn = norm_cast(memb, mem_norm[layer])
            h = cross_attn(xn, memn, xa_wq[layer], xa_wk[layer], xa_wv[layer], xa_wo[layer])
            xb, xn = resid_norm(xb, h, xa_norm_post[layer], ffn_norm_pre[layer])
            h = conv_ffn(xn, ffn_w_gate[layer], ffn_w_up[layer], ffn_conv_w[layer],
                         ffn_conv_b[layer], ffn_w_down[layer])
            if layer + 1 < depth:
                xb, xn = resid_norm(xb, h, ffn_norm_post[layer], mix_norm_pre[layer + 1])
            else:
                xb = resid_norm(xb, h, ffn_norm_post[layer])
        outs.append(xb)
    return jnp.stack(outs, axis=0)
```
